```python
import math
import jax, jax.numpy as jnp
from jax import lax
import numpy as np

D_MODEL = 2048
BATCH = 2
SEQ = 16384
DEPTH = 4

N_A = DEPTH // 2
N_B = DEPTH - N_A
DA_HEADS = 8
DA_HEAD_DIM = 128
DA_ROT = DA_HEAD_DIM // 4
DA_QKV = 3 * DA_HEADS * 2 * DA_HEAD_DIM
MLA_HEADS = 16
Q_LORA = 384
KV_LORA = 512
QK_NOPE = 128
QK_ROPE = 64
V_DIM = 128
D_FF = 4 * D_MODEL
ROPE_THETA = 500000.0
Q_BLOCK = 128
NORM_EPS = 1e-6
SUBLN_EPS = 1e-5

kernel_name = "yoco_diffattn_mla_hybrid"


def rms_norm(x, g, eps=NORM_EPS):
    xf = x.astype(jnp.float32)
    y = xf * lax.rsqrt(jnp.mean(xf * xf, axis=-1, keepdims=True) + eps)
    return (y * g.astype(jnp.float32)).astype(x.dtype)


def rope_cos_sin(positions, rot_dim, dtype):
    inv = ROPE_THETA ** (-jnp.arange(0, rot_dim, 2, dtype=jnp.float32) / rot_dim)
    ang = positions.astype(jnp.float32)[..., None] * inv
    return jnp.cos(ang).astype(dtype), jnp.sin(ang).astype(dtype)


def apply_rope(x, cos, sin):
    x1, x2 = jnp.split(x, 2, axis=-1)
    return jnp.concatenate([x1 * cos - x2 * sin, x2 * cos + x1 * sin], axis=-1)


def partial_rope(x, cos, sin, rot):
    return jnp.concatenate([apply_rope(x[..., :rot], cos, sin), x[..., rot:]], axis=-1)


def to_blocks(t):
    B, S = t.shape[:2]
    return jnp.moveaxis(t.reshape((B, S // Q_BLOCK, Q_BLOCK) + t.shape[2:]), 1, 0)


def from_blocks(t):
    t = jnp.moveaxis(t, 0, 1)
    return t.reshape((t.shape[0], t.shape[1] * t.shape[2]) + t.shape[3:])


def causal_softmax(s, blk_idx, seq):
    q_pos = blk_idx * Q_BLOCK + jnp.arange(Q_BLOCK)
    mask = jnp.arange(seq)[None, :] <= q_pos[:, None]
    return jax.nn.softmax(jnp.where(mask, s, -jnp.inf), axis=-1)


def diff_attention_core(q1, q2, k1, k2, v, lam):
    S = q1.shape[1]
    scale = DA_HEAD_DIM ** -0.5

    def block(args):
        qa, qb, i = args
        s1 = jnp.einsum('bqhd,bkhd->bhqk', qa, k1, preferred_element_type=jnp.float32) * scale
        s2 = jnp.einsum('bqhd,bkhd->bhqk', qb, k2, preferred_element_type=jnp.float32) * scale
        a = causal_softmax(s1, i, S) - lam * causal_softmax(s2, i, S)
        return jnp.einsum('bhqk,bkhe->bqhe', a.astype(v.dtype), v)

    out = lax.map(block, (to_blocks(q1), to_blocks(q2), jnp.arange(S // Q_BLOCK)))
    return from_blocks(out)


def diff_attention(h, w_qkv, lam_vecs, subln_g, w_o, cos, sin, layer_idx):
    B, S, _ = h.shape
    n_qk = DA_HEADS * 2 * DA_HEAD_DIM
    q, k, v = jnp.split(h @ w_qkv, [n_qk, 2 * n_qk], axis=-1)
    c, s_ = cos[:, :, None, :], sin[:, :, None, :]
    q = partial_rope(q.reshape(B, S, 2 * DA_HEADS, DA_HEAD_DIM), c, s_, DA_ROT)
    k = partial_rope(k.reshape(B, S, 2 * DA_HEADS, DA_HEAD_DIM), c, s_, DA_ROT)
    q = q.reshape(B, S, DA_HEADS, 2, DA_HEAD_DIM)
    k = k.reshape(B, S, DA_HEADS, 2, DA_HEAD_DIM)
    v = v.reshape(B, S, DA_HEADS, 2 * DA_HEAD_DIM)
    lam_init = 0.8 - 0.6 * math.exp(-0.3 * layer_idx)
    lv = lam_vecs.astype(jnp.float32)
    lam = jnp.exp(jnp.sum(lv[0] * lv[1])) - jnp.exp(jnp.sum(lv[2] * lv[3])) + lam_init
    o = diff_attention_core(q[..., 0, :], q[..., 1, :], k[..., 0, :], k[..., 1, :], v, lam)
    o = rms_norm(o, subln_g, eps=SUBLN_EPS) * (1.0 - lam_init)
    return o.reshape(B, S, DA_HEADS * 2 * DA_HEAD_DIM) @ w_o


def shared_latent_kv(x, kv_in_norm_g, w_dkv, kv_norm_g, w_ukv, cos, sin):
    B, S, _ = x.shape
    kv_a = rms_norm(x, kv_in_norm_g) @ w_dkv
    c_kv = rms_norm(kv_a[..., :KV_LORA], kv_norm_g)
    k_rope = apply_rope(kv_a[..., KV_LORA:], cos, sin)
    kv = (c_kv @ w_ukv).reshape(B, S, MLA_HEADS, QK_NOPE + V_DIM)
    return kv[..., :QK_NOPE], k_rope, kv[..., QK_NOPE:]


def mla_core(q_nope, q_rope, k_nope, k_rope, v):
    S = q_nope.shape[1]
    scale = (QK_NOPE + QK_ROPE) ** -0.5

    def block(args):
        qn, qr, i = args
        s = (jnp.einsum('bqhd,bkhd->bhqk', qn, k_nope, preferred_element_type=jnp.float32)
             + jnp.einsum('bqhr,bkr->bhqk', qr, k_rope, preferred_element_type=jnp.float32)) * scale
        p = causal_softmax(s, i, S)
        return jnp.einsum('bhqk,bkhd->bqhd', p.astype(v.dtype), v)

    out = lax.map(block, (to_blocks(q_nope), to_blocks(q_rope), jnp.arange(S // Q_BLOCK)))
    return from_blocks(out)


def mla_attention(h, w_dq, q_norm_g, w_uq, w_o, k_nope, k_rope, v, cos, sin):
    B, S, _ = h.shape
    cq = rms_norm(h @ w_dq, q_norm_g)
    q = (cq @ w_uq).reshape(B, S, MLA_HEADS, QK_NOPE + QK_ROPE)
    q_rope = apply_rope(q[..., QK_NOPE:], cos[:, :, None, :], sin[:, :, None, :])
    o = mla_core(q[..., :QK_NOPE], q_rope, k_nope, k_rope, v)
    return o.reshape(B, S, MLA_HEADS * V_DIM) @ w_o


def sq_relu_mlp(h, w_up, w_down):
    return jnp.square(jax.nn.relu(h @ w_up)) @ w_down


def setup_inputs(seed: int = 0) -> dict:
    key = jax.random.key(seed)
    ks = jax.random.split(key, 24)
    f32 = jnp.float32

    def w(k, shape, fan_in):
        return jax.random.normal(k, shape, f32) * (fan_in ** -0.5)

    def gain(k, shape):
        return 1.0 + 0.02 * jax.random.normal(k, shape, f32)

    return {
        "x": jax.random.normal(ks[0], (BATCH, SEQ, D_MODEL), f32),
        "positions": jnp.broadcast_to(jnp.arange(SEQ, dtype=jnp.int32), (BATCH, SEQ)),
        "a_norm_g": gain(ks[1], (N_A, D_MODEL)),
        "a_w_qkv": w(ks[2], (N_A, D_MODEL, DA_QKV), D_MODEL),
        "a_lambda": 0.1 * jax.random.normal(ks[3], (N_A, 4, DA_HEAD_DIM), f32),
        "a_subln_g": gain(ks[4], (N_A, 2 * DA_HEAD_DIM)),
        "a_w_o": w(ks[5], (N_A, DA_HEADS * 2 * DA_HEAD_DIM, D_MODEL), DA_HEADS * 2 * DA_HEAD_DIM),
        "kv_in_norm_g": gain(ks[6], (D_MODEL,)),
        "w_dkv": w(ks[7], (D_MODEL, KV_LORA + QK_ROPE), D_MODEL),
        "kv_norm_g": gain(ks[8], (KV_LORA,)),
        "w_ukv": w(ks[9], (KV_LORA, MLA_HEADS * (QK_NOPE + V_DIM)), KV_LORA),
        "b_norm_g": gain(ks[10], (N_B, D_MODEL)),
        "b_w_dq": w(ks[11], (N_B, D_MODEL, Q_LORA), D_MODEL),
        "b_q_norm_g": gain(ks[12], (N_B, Q_LORA)),
        "b_w_uq": w(ks[13], (N_B, Q_LORA, MLA_HEADS * (QK_NOPE + QK_ROPE)), Q_LORA),
        "b_w_o": w(ks[14], (N_B, MLA_HEADS * V_DIM, D_MODEL), MLA_HEADS * V_DIM),
        "mlp_norm_g": gain(ks[15], (DEPTH, D_MODEL)),
        "w_up": w(ks[16], (DEPTH, D_MODEL, D_FF), D_MODEL),
        "w_down": w(ks[17], (DEPTH, D_FF, D_MODEL), D_FF),
        "final_norm_g": gain(ks[18], (D_MODEL,)),
    }


def reference(x, positions, a_norm_g, a_w_qkv, a_lambda, a_subln_g, a_w_o,
              kv_in_norm_g, w_dkv, kv_norm_g, w_ukv,
              b_norm_g, b_w_dq, b_q_norm_g, b_w_uq, b_w_o,
              mlp_norm_g, w_up, w_down, final_norm_g):
    cos_a, sin_a = rope_cos_sin(positions, DA_ROT, x.dtype)
    cos_b, sin_b = rope_cos_sin(positions, QK_ROPE, x.dtype)
    k_nope = k_rope = v = None
    for l in range(DEPTH):
        if l < N_A:
            x = x + diff_attention(rms_norm(x, a_norm_g[l]), a_w_qkv[l], a_lambda[l],
                                   a_subln_g[l], a_w_o[l], cos_a, sin_a, l)
        else:
            j = l - N_A
            if j == 0:
                k_nope, k_rope, v = shared_latent_kv(x, kv_in_norm_g, w_dkv, kv_norm_g,
                                                     w_ukv, cos_b, sin_b)
            x = x + mla_attention(rms_norm(x, b_norm_g[j]), b_w_dq[j], b_q_norm_g[j],
                                  b_w_uq[j], b_w_o[j], k_nope, k_rope, v, cos_b, sin_b)
        x = x + sq_relu_mlp(rms_norm(x, mlp_norm_g[l]), w_up[l], w_down[l])
    return rms_norm(x, final_norm_g)
```

```python
import functools
import math

import jax
import jax.numpy as jnp
from jax import lax
from jax.experimental import pallas as pl
from jax.experimental.pallas import tpu as pltpu

D_MODEL = 2048
DEPTH = 4
N_A = DEPTH // 2
DA_HEADS = 8
DA_HEAD_DIM = 128
DA_ROT = DA_HEAD_DIM // 4
MLA_HEADS = 16
Q_LORA = 384
KV_LORA = 512
QK_NOPE = 128
QK_ROPE = 64
V_DIM = 128
D_FF = 4 * D_MODEL
ROPE_THETA = 500000.0
NORM_EPS = 1e-6
SUBLN_EPS = 1e-5

LANES = 128
VMEM_LIMIT_BYTES = 56 * 1024 * 1024

MASK_VALUE = -1e30

ROW_BLOCK = 512
COL_BLOCK = 512
FF_BLOCK = 512
ATT_BLOCK = 512

_NT = (((1,), (1,)), ((), ()))


def _params(sem):
    return pltpu.CompilerParams(dimension_semantics=sem, vmem_limit_bytes=VMEM_LIMIT_BYTES)


def _rms(x, g, eps):
    return x * lax.rsqrt(jnp.mean(x * x, axis=-1, keepdims=True) + eps) * g


def _rope_group(x, c, sa, sb, half):
    return (x * c + pltpu.roll(x, LANES - half, 1) * sa + pltpu.roll(x, half, 1) * sb)


def _rope_tables(positions, rot):
    half = rot // 2
    inv = ROPE_THETA ** (-jnp.arange(0, rot, 2, dtype=jnp.float32) / rot)
    ang = positions.reshape(-1).astype(jnp.float32)[:, None] * inv
    cos, sin = jnp.cos(ang), jnp.sin(ang)
    t = ang.shape[0]
    c = jnp.concatenate([cos, cos, jnp.ones((t, LANES - rot), jnp.float32)], axis=1)
    sa = jnp.concatenate([-sin, jnp.zeros((t, LANES - half), jnp.float32)], axis=1)
    sb = jnp.concatenate([jnp.zeros((t, half), jnp.float32), sin,
                          jnp.zeros((t, LANES - rot), jnp.float32)], axis=1)
    return jnp.stack([c, sa, sb])


def _qkv_kernel(x_ref, g_ref, w_ref, rope_ref, o_ref, xn_ref, *, n_rope_blocks, half):
    j = pl.program_id(1)

    @pl.when(j == 0)
    def _():
        xn_ref[...] = _rms(x_ref[...], g_ref[...], NORM_EPS).astype(xn_ref.dtype)

    acc = jnp.dot(xn_ref[...], w_ref[...], preferred_element_type=jnp.float32)

    @pl.when(j < n_rope_blocks)
    def _():
        c, sa, sb = rope_ref[0], rope_ref[1], rope_ref[2]
        for grp in range(acc.shape[1] // LANES):
            sl = slice(grp * LANES, (grp + 1) * LANES)
            o_ref[:, sl] = _rope_group(acc[:, sl], c, sa, sb, half).astype(o_ref.dtype)

    @pl.when(j >= n_rope_blocks)
    def _():
        o_ref[...] = acc.astype(o_ref.dtype)


def _qkv_proj(x, g, w, rope, n_rope_cols, half):
    t, d = x.shape
    n = w.shape[1]
    grid = (t // ROW_BLOCK, n // COL_BLOCK)
    return pl.pallas_call(
        functools.partial(_qkv_kernel, n_rope_blocks=n_rope_cols // COL_BLOCK, half=half),
        grid=grid,
        in_specs=[
            pl.BlockSpec((ROW_BLOCK, d), lambda i, j: (i, 0)),
            pl.BlockSpec((1, d), lambda i, j: (0, 0)),
            pl.BlockSpec((d, COL_BLOCK), lambda i, j: (0, j)),
            pl.BlockSpec((3, ROW_BLOCK, LANES), lambda i, j: (0, i, 0)),
        ],
        out_specs=pl.BlockSpec((ROW_BLOCK, COL_BLOCK), lambda i, j: (i, j)),
        out_shape=jax.ShapeDtypeStruct((t, n), jnp.bfloat16),
        scratch_shapes=[pltpu.VMEM((ROW_BLOCK, d), jnp.bfloat16)],
        compiler_params=_params(("parallel", "arbitrary")),
        name="da_qkv_proj",
    )(x, g, w, rope)


def _softmax_step(s, v, m_ref, l_ref, acc_ref):
    bk = s.shape[1]
    dv = acc_ref.shape[1]
    m_prev = m_ref[...]
    m_next = jnp.maximum(m_prev, jnp.max(s, axis=1, keepdims=True))
    p = jnp.exp(s - pltpu.repeat(m_next, bk // LANES, axis=1))
    alpha = jnp.exp(m_prev - m_next)
    l_ref[...] = alpha * l_ref[...] + jnp.sum(p, axis=1, keepdims=True)
    m_ref[...] = m_next
    pv = jnp.dot(p.astype(v.dtype), v, preferred_element_type=jnp.float32)
    acc_ref[...] = acc_ref[...] * pltpu.repeat(alpha, dv // LANES, axis=1) + pv


def _causal_mask(s):
    row = lax.broadcasted_iota(jnp.int32, s.shape, 0)
    col = lax.broadcasted_iota(jnp.int32, s.shape, 1)
    return jnp.where(col <= row, s, MASK_VALUE)


def _init_stats(m_ref, l_ref, acc_ref):
    m_ref[...] = jnp.full(m_ref.shape, MASK_VALUE, jnp.float32)
    l_ref[...] = jnp.zeros(l_ref.shape, jnp.float32)
    acc_ref[...] = jnp.zeros(acc_ref.shape, jnp.float32)


def _diff_attn_kernel(q_ref, k_ref, v_ref, lam_ref, g_ref, o_ref,
                      m1_ref, l1_ref, acc1_ref, m2_ref, l2_ref, acc2_ref, *, lam_init):
    qi = pl.program_id(2)
    d = DA_HEAD_DIM
    q1 = q_ref[:, :d]
    q2 = q_ref[:, d:]
    _init_stats(m1_ref, l1_ref, acc1_ref)
    _init_stats(m2_ref, l2_ref, acc2_ref)

    def block(kj, masked):
        start = pl.multiple_of(kj * ATT_BLOCK, ATT_BLOCK)
        k = k_ref[pl.ds(start, ATT_BLOCK), :]
        v = v_ref[pl.ds(start, ATT_BLOCK), :]
        s1 = lax.dot_general(q1, k[:, :d], _NT, preferred_element_type=jnp.float32)
        s2 = lax.dot_general(q2, k[:, d:], _NT, preferred_element_type=jnp.float32)
        if masked:
            s1 = _causal_mask(s1)
            s2 = _causal_mask(s2)
        _softmax_step(s1, v, m1_ref, l1_ref, acc1_ref)
        _softmax_step(s2, v, m2_ref, l2_ref, acc2_ref)

    def body(kj, carry):
        block(kj, False)
        return carry

    lax.fori_loop(0, qi, body, 0)
    block(qi, True)

    lv = lam_ref[...]
    lam = (jnp.exp(jnp.sum(lv[0:1] * lv[1:2], axis=1, keepdims=True))
           - jnp.exp(jnp.sum(lv[2:3] * lv[3:4], axis=1, keepdims=True)) + lam_init)
    dv = acc1_ref.shape[1]
    o1 = acc1_ref[...] / pltpu.repeat(l1_ref[...], dv // LANES, axis=1)
    o2 = acc2_ref[...] / pltpu.repeat(l2_ref[...], dv // LANES, axis=1)
    o = o1 - lam * o2
    o_ref[...] = (_rms(o, g_ref[...], SUBLN_EPS) * (1.0 - lam_init)).astype(o_ref.dtype)


def _diff_attention(qkv, lam_vecs, subln_g, batch, seq, layer_idx):
    t = qkv.shape[0]
    hw = 2 * DA_HEAD_DIM
    nq = seq // ATT_BLOCK
    lam_init = 0.8 - 0.6 * math.exp(-0.3 * layer_idx)
    stat = pltpu.VMEM((ATT_BLOCK, LANES), jnp.float32)
    acc = pltpu.VMEM((ATT_BLOCK, hw), jnp.float32)
    return pl.pallas_call(
        functools.partial(_diff_attn_kernel, lam_init=lam_init),
        grid=(batch, DA_HEADS, nq),
        in_specs=[
            pl.BlockSpec((ATT_BLOCK, hw), lambda b, h, i: (b * nq + i, h)),
            pl.BlockSpec((seq, hw), lambda b, h, i: (b, DA_HEADS + h)),
            pl.BlockSpec((seq, hw), lambda b, h, i: (b, 2 * DA_HEADS + h)),
            pl.BlockSpec((4, DA_HEAD_DIM), lambda b, h, i: (0, 0)),
            pl.BlockSpec((1, hw), lambda b, h, i: (0, 0)),
        ],
        out_specs=pl.BlockSpec((ATT_BLOCK, hw), lambda b, h, i: (b * nq + i, h)),
        out_shape=jax.ShapeDtypeStruct((t, DA_HEADS * hw), jnp.bfloat16),
        scratch_shapes=[stat, stat, acc, stat, stat, acc],
        compiler_params=_params(("parallel", "parallel", "parallel")),
        name="diff_attention",
    )(qkv, qkv, qkv, lam_vecs, subln_g)


def _proj_residual_kernel(a_ref, w_ref, x_ref, o_ref):
    o_ref[...] = x_ref[...] + jnp.dot(a_ref[...], w_ref[...], preferred_element_type=jnp.float32)


def _proj_residual(a, w, x):
    t, k = a.shape
    n = w.shape[1]
    return pl.pallas_call(
        _proj_residual_kernel,
        grid=(t // ROW_BLOCK, n // COL_BLOCK),
        in_specs=[
            pl.BlockSpec((ROW_BLOCK, k), lambda i, j: (i, 0)),
            pl.BlockSpec((k, COL_BLOCK), lambda i, j: (0, j)),
            pl.BlockSpec((ROW_BLOCK, COL_BLOCK), lambda i, j: (i, j)),
        ],
        out_specs=pl.BlockSpec((ROW_BLOCK, COL_BLOCK), lambda i, j: (i, j)),
        out_shape=jax.ShapeDtypeStruct((t, n), jnp.float32),
        compiler_params=_params(("parallel", "parallel")),
        name="proj_residual",
    )(a, w, x)


def _mlp_kernel(x_ref, g_ref, wu_ref, wd_ref, gf_ref, o_ref, xn_ref, acc_ref, *, final_norm):
    f = pl.program_id(1)

    @pl.when(f == 0)
    def _():
        xn_ref[...] = _rms(x_ref[...], g_ref[...], NORM_EPS).astype(xn_ref.dtype)
        acc_ref[...] = jnp.zeros(acc_ref.shape, jnp.float32)

    h = jnp.dot(xn_ref[...], wu_ref[...], preferred_element_type=jnp.float32)
    h = jnp.square(jnp.maximum(h, 0.0)).astype(wd_ref.dtype)
    acc_ref[...] += jnp.dot(h, wd_ref[...], preferred_element_type=jnp.float32)

    @pl.when(f == pl.num_programs(1) - 1)
    def _():
        y = x_ref[...] + acc_ref[...]
        if final_norm:
            y = _rms(y, gf_ref[...], NORM_EPS)
        o_ref[...] = y


def _mlp(x, g, w_up, w_down, g_final, final_norm):
    t, d = x.shape
    ff = w_up.shape[1]
    return pl.pallas_call(
        functools.partial(_mlp_kernel, final_norm=final_norm),
        grid=(t // ROW_BLOCK, ff // FF_BLOCK),
        in_specs=[
            pl.BlockSpec((ROW_BLOCK, d), lambda i, f: (i, 0)),
            pl.BlockSpec((1, d), lambda i, f: (0, 0)),
            pl.BlockSpec((d, FF_BLOCK), lambda i, f: (0, f)),
            pl.BlockSpec((FF_BLOCK, d), lambda i, f: (f, 0)),
            pl.BlockSpec((1, d), lambda i, f: (0, 0)),
        ],
        out_specs=pl.BlockSpec((ROW_BLOCK, d), lambda i, f: (i, 0)),
        out_shape=jax.ShapeDtypeStruct((t, d), jnp.float32),
        scratch_shapes=[pltpu.VMEM((ROW_BLOCK, d), jnp.bfloat16),
                        pltpu.VMEM((ROW_BLOCK, d), jnp.float32)],
        compiler_params=_params(("parallel", "arbitrary")),
        name="sq_relu_mlp",
    )(x, g, w_up, w_down, g_final)


def _latent_kv_kernel(x_ref, gin_ref, wd_ref, gkv_ref, wu_ref, rope_ref, kv_ref, kr_ref):
    xn = _rms(x_ref[...], gin_ref[...], NORM_EPS).astype(wd_ref.dtype)
    kv_a = jnp.dot(xn, wd_ref[...], preferred_element_type=jnp.float32)
    c_kv = _rms(kv_a[:, :KV_LORA], gkv_ref[...], NORM_EPS).astype(wu_ref.dtype)
    kr = _rope_group(kv_a[:, KV_LORA:], rope_ref[0], rope_ref[1], rope_ref[2], QK_ROPE // 2)
    kr_ref[...] = kr.astype(kr_ref.dtype)
    kv_ref[...] = jnp.dot(c_kv, wu_ref[...], preferred_element_type=jnp.float32).astype(kv_ref.dtype)


def _latent_kv(x, g_in, w_dkv, g_kv, w_ukv, rope):
    t, d = x.shape
    na = w_dkv.shape[1]
    n = w_ukv.shape[1]
    return pl.pallas_call(
        _latent_kv_kernel,
        grid=(t // ROW_BLOCK,),
        in_specs=[
            pl.BlockSpec((ROW_BLOCK, d), lambda i: (i, 0)),
            pl.BlockSpec((1, d), lambda i: (0, 0)),
            pl.BlockSpec((d, na), lambda i: (0, 0)),
            pl.BlockSpec((1, KV_LORA), lambda i: (0, 0)),
            pl.BlockSpec((KV_LORA, n), lambda i: (0, 0)),
            pl.BlockSpec((3, ROW_BLOCK, LANES), lambda i: (0, i, 0)),
        ],
        out_specs=[pl.BlockSpec((ROW_BLOCK, n), lambda i: (i, 0)),
                   pl.BlockSpec((ROW_BLOCK, LANES), lambda i: (i, 0))],
        out_shape=[jax.ShapeDtypeStruct((t, n), jnp.bfloat16),
                   jax.ShapeDtypeStruct((t, LANES), jnp.bfloat16)],
        compiler_params=_params(("parallel",)),
        name="mla_latent_kv",
    )(x, g_in, w_dkv, g_kv, w_ukv, rope)


def _mla_q_kernel(x_ref, g_ref, wd_ref, gq_ref, wu_ref, rope_ref, o_ref):
    xn = _rms(x_ref[...], g_ref[...], NORM_EPS).astype(wd_ref.dtype)
    cq = jnp.dot(xn, wd_ref[...], preferred_element_type=jnp.float32)
    cq = _rms(cq, gq_ref[...], NORM_EPS).astype(wu_ref.dtype)
    q = jnp.dot(cq, wu_ref[...], preferred_element_type=jnp.float32)
    n_nope = MLA_HEADS * QK_NOPE
    o_ref[:, :n_nope] = q[:, :n_nope].astype(o_ref.dtype)
    c, sa, sb = rope_ref[0], rope_ref[1], rope_ref[2]
    for h in range(MLA_HEADS):
        sl = slice(n_nope + h * LANES, n_nope + (h + 1) * LANES)
        o_ref[:, sl] = _rope_group(q[:, sl], c, sa, sb, QK_ROPE // 2).astype(o_ref.dtype)


def _mla_q(x, g, w_dq, g_q, w_uq, rope):
    t, d = x.shape
    n = w_uq.shape[1]
    return pl.pallas_call(
        _mla_q_kernel,
        grid=(t // ROW_BLOCK,),
        in_specs=[
            pl.BlockSpec((ROW_BLOCK, d), lambda i: (i, 0)),
            pl.BlockSpec((1, d), lambda i: (0, 0)),
            pl.BlockSpec((d, Q_LORA), lambda i: (0, 0)),
            pl.BlockSpec((1, Q_LORA), lambda i: (0, 0)),
            pl.BlockSpec((Q_LORA, n), lambda i: (0, 0)),
            pl.BlockSpec((3, ROW_BLOCK, LANES), lambda i: (0, i, 0)),
        ],
        out_specs=pl.BlockSpec((ROW_BLOCK, n), lambda i: (i, 0)),
        out_shape=jax.ShapeDtypeStruct((t, n), jnp.bfloat16),
        compiler_params=_params(("parallel",)),
        name="mla_q_proj",
    )(x, g, w_dq, g_q, w_uq, rope)


def _mla_attn_kernel(qn_ref, qr_ref, kn_ref, kr_ref, v_ref, o_ref, m_ref, l_ref, acc_ref):
    qi = pl.program_id(2)
    q = jnp.concatenate([qn_ref[...], qr_ref[...]], axis=1)
    _init_stats(m_ref, l_ref, acc_ref)

    def block(kj, masked):
        start = pl.multiple_of(kj * ATT_BLOCK, ATT_BLOCK)
        k = jnp.concatenate([kn_ref[pl.ds(start, ATT_BLOCK), :],
                             kr_ref[pl.ds(start, ATT_BLOCK), :]], axis=1)
        v = v_ref[pl.ds(start, ATT_BLOCK), :]
        s = lax.dot_general(q, k, _NT, preferred_element_type=jnp.float32)
        if masked:
            s = _causal_mask(s)
        _softmax_step(s, v, m_ref, l_ref, acc_ref)

    def body(kj, carry):
        block(kj, False)
        return carry

    lax.fori_loop(0, qi, body, 0)
    block(qi, True)
    o_ref[...] = (acc_ref[...] / l_ref[...]).astype(o_ref.dtype)


def _mla_attention(q, kv, k_rope, batch, seq):
    t = q.shape[0]
    nq = seq // ATT_BLOCK
    hh = MLA_HEADS
    return pl.pallas_call(
        _mla_attn_kernel,
        grid=(batch, hh, nq),
        in_specs=[
            pl.BlockSpec((ATT_BLOCK, LANES), lambda b, h, i: (b * nq + i, h)),
            pl.BlockSpec((ATT_BLOCK, LANES), lambda b, h, i: (b * nq + i, hh + h)),
            pl.BlockSpec((seq, LANES), lambda b, h, i: (b, h)),
            pl.BlockSpec((seq, LANES), lambda b, h, i: (b, 0)),
            pl.BlockSpec((seq, LANES), lambda b, h, i: (b, hh + h)),
        ],
        out_specs=pl.BlockSpec((ATT_BLOCK, LANES), lambda b, h, i: (b * nq + i, h)),
        out_shape=jax.ShapeDtypeStruct((t, hh * V_DIM), jnp.bfloat16),
        scratch_shapes=[pltpu.VMEM((ATT_BLOCK, LANES), jnp.float32),
                        pltpu.VMEM((ATT_BLOCK, LANES), jnp.float32),
                        pltpu.VMEM((ATT_BLOCK, V_DIM), jnp.float32)],
        compiler_params=_params(("parallel", "parallel", "parallel")),
        name="mla_attention",
    )(q, q, kv, k_rope, kv)


def _prep_w_qkv(w):
    n_qk = DA_HEADS * 2 * DA_HEAD_DIM
    scale = DA_HEAD_DIM ** -0.5
    return jnp.concatenate([w[:, :n_qk] * scale, w[:, n_qk:]], axis=1).astype(jnp.bfloat16)


def _prep_w_uq(w):
    scale = (QK_NOPE + QK_ROPE) ** -0.5
    w = w.reshape(Q_LORA, MLA_HEADS, QK_NOPE + QK_ROPE) * scale
    nope = w[:, :, :QK_NOPE].reshape(Q_LORA, MLA_HEADS * QK_NOPE)
    rope = jnp.pad(w[:, :, QK_NOPE:], ((0, 0), (0, 0), (0, LANES - QK_ROPE)))
    return jnp.concatenate([nope, rope.reshape(Q_LORA, MLA_HEADS * LANES)], axis=1).astype(jnp.bfloat16)


def _prep_w_ukv(w):
    w = w.reshape(KV_LORA, MLA_HEADS, QK_NOPE + V_DIM)
    kn = w[:, :, :QK_NOPE].reshape(KV_LORA, MLA_HEADS * QK_NOPE)
    v = w[:, :, QK_NOPE:].reshape(KV_LORA, MLA_HEADS * V_DIM)
    return jnp.concatenate([kn, v], axis=1).astype(jnp.bfloat16)


def _prep_w_dkv(w):
    return jnp.pad(w, ((0, 0), (0, LANES - QK_ROPE))).astype(jnp.bfloat16)


def kernel(x, positions, a_norm_g, a_w_qkv, a_lambda, a_subln_g, a_w_o, kv_in_norm_g, w_dkv, kv_norm_g,
           w_ukv, b_norm_g, b_w_dq, b_q_norm_g, b_w_uq, b_w_o, mlp_norm_g, w_up, w_down, final_norm_g):
    batch, seq, d = x.shape
    bf16 = jnp.bfloat16
    xs = x.reshape(batch * seq, d)
    rope_a = _rope_tables(positions, DA_ROT)
    rope_b = _rope_tables(positions, QK_ROPE)
    row = lambda g: g.reshape(1, -1)

    kv = k_rope = None
    for l in range(DEPTH):
        if l < N_A:
            qkv = _qkv_proj(xs, row(a_norm_g[l]), _prep_w_qkv(a_w_qkv[l]), rope_a,
                            2 * DA_HEADS * 2 * DA_HEAD_DIM, DA_ROT // 2)
            o = _diff_attention(qkv, a_lambda[l], row(a_subln_g[l]), batch, seq, l)
            xs = _proj_residual(o, a_w_o[l].astype(bf16), xs)
        else:
            j = l - N_A
            if j == 0:
                kv, k_rope = _latent_kv(xs, row(kv_in_norm_g), _prep_w_dkv(w_dkv), row(kv_norm_g),
                                        _prep_w_ukv(w_ukv), rope_b)
            q = _mla_q(xs, row(b_norm_g[j]), b_w_dq[j].astype(bf16), row(b_q_norm_g[j]),
                       _prep_w_uq(b_w_uq[j]), rope_b)
            o = _mla_attention(q, kv, k_rope, batch, seq)
            xs = _proj_residual(o, b_w_o[j].astype(bf16), xs)
        xs = _mlp(xs, row(mlp_norm_g[l]), w_up[l].astype(bf16), w_down[l].astype(bf16),
                  row(final_norm_g), l == DEPTH - 1)
    return xs.reshape(batch, seq, d)
```
